```python
import math
import jax, jax.numpy as jnp
from jax import lax
import numpy as np

D_MODEL = 1024
BATCH = 4
SEQ = 4096
DEPTH = 1
DEC_BATCH = 32
DEC_SEQ = 4
PAST_LEN = 8192
PAGE_SIZE = 128

HEAD_DIM = 64
NSA_HEADS = 8
NSA_KV_HEADS = 2
NSA_GROUP = NSA_HEADS // NSA_KV_HEADS
CMP_LEN = 32
CMP_STRIDE = 16
CMP_RATIO = CMP_LEN // CMP_STRIDE
CMP_HIDDEN = 64
SLC_BLOCK = 64
SLC_RATIO = SLC_BLOCK // CMP_STRIDE
N_SELECT = 16
WINDOW = 512
Q_BLOCK = 128
N_KV_SLOTS = 6
GLA_HEADS = 4
GLA_DK = 64
GLA_DV = 128
GLA_LOWRANK = 16
GLA_NORMALIZER = 16.0
GLA_CHUNK = 32
N_BUCKETS = 32
MAX_DISTANCE = 128
D_FF = ((8 * D_MODEL + 767) // 768) * 256
D_PLE = 256
EPS = 1e-6
NEG_INF = -1e30
FORCED_SCORE = 1e9

NSA_Q_DIM = NSA_HEADS * HEAD_DIM
NSA_KV_DIM = N_KV_SLOTS * NSA_KV_HEADS * HEAD_DIM
NSA_GATE_DIM = 3 * NSA_HEADS
GLA_QK_DIM = GLA_HEADS * GLA_DK
GLA_V_DIM = GLA_HEADS * GLA_DV
MIX_WIDTH = NSA_Q_DIM + GLA_V_DIM
IN_SPLITS = (NSA_Q_DIM, NSA_KV_DIM, NSA_GATE_DIM, GLA_QK_DIM, GLA_QK_DIM, GLA_V_DIM, GLA_LOWRANK, GLA_V_DIM)
IN_DIM = sum(IN_SPLITS)

kernel_name = "nsa_gla_hybrid_step"


def rmsnorm(x, g):
    xf = x.astype(jnp.float32)
    xf = xf * lax.rsqrt(jnp.mean(xf * xf, axis=-1, keepdims=True) + EPS)
    return xf.astype(x.dtype) * g


def t5_bucket(dist):
    n = jnp.maximum(dist, 0)
    max_exact = N_BUCKETS // 2
    nf = jnp.maximum(n, 1).astype(jnp.float32)
    large = max_exact + (jnp.log(nf / max_exact) / math.log(MAX_DISTANCE / max_exact)
                         * (N_BUCKETS - max_exact)).astype(jnp.int32)
    large = jnp.minimum(large, N_BUCKETS - 1)
    return jnp.where(n < max_exact, n, large)


def split_projection(xn, w_in):
    B, T, _ = xn.shape
    offs = np.cumsum(IN_SPLITS)[:-1].tolist()
    q_n, kv, gt, q_g, k_g, v_g, lr, g_g = jnp.split(xn @ w_in, offs, axis=-1)
    q_n = q_n.reshape(B, T, NSA_KV_HEADS, NSA_GROUP, HEAD_DIM)
    kv = kv.reshape(B, T, N_KV_SLOTS, NSA_KV_HEADS, HEAD_DIM)
    gt = jax.nn.sigmoid(gt).reshape(B, T, 3, NSA_KV_HEADS, NSA_GROUP, 1)
    q_g = q_g.reshape(B, T, GLA_HEADS, GLA_DK) * (GLA_DK ** -0.5)
    k_g = k_g.reshape(B, T, GLA_HEADS, GLA_DK)
    v_g = v_g.reshape(B, T, GLA_HEADS, GLA_DV)
    g_g = g_g.reshape(B, T, GLA_HEADS, GLA_DV)
    return q_n, kv, gt, q_g, k_g, v_g, lr, g_g


def compress_rows(rows, pe, w1, w2):
    B, T, G, HD = rows.shape
    nb = T // CMP_STRIDE
    sub = rows[:, :nb * CMP_STRIDE].reshape(B, nb, CMP_STRIDE, G, HD)
    pe_r = pe.reshape(CMP_RATIO, CMP_STRIDE, HD)
    w1_r = w1.reshape(CMP_RATIO, CMP_STRIDE, HD, CMP_HIDDEN)
    n_cmp = nb - CMP_RATIO + 1
    h = sum(jnp.einsum('bnsgd,sdh->bngh', sub[:, j:j + n_cmp] + pe_r[j][:, None, :], w1_r[j])
            for j in range(CMP_RATIO))
    return jnp.einsum('bngh,hd->bngd', jax.nn.gelu(h), w2)


def nsa_query_block(q, pos, k_cmp, v_cmp, k_slc, v_slc, rel_bias):
    B, Q, G, R, _ = q.shape
    scale = HEAD_DIM ** -0.5
    n_cmp = k_cmp.shape[1]
    n_slc = k_slc.shape[2]
    blk_end = jnp.arange(n_cmp, dtype=jnp.int32) * CMP_STRIDE + (CMP_LEN - 1)
    dist_c = pos[:, None] - blk_end[None, :]
    vis_c = (dist_c >= 0)[None, :, :, None, None]
    bias_c = rel_bias[t5_bucket(dist_c)].reshape(Q, n_cmp, G, R)
    s_c = jnp.einsum('bqgrd,bngd->bqngr', q, k_cmp).astype(jnp.float32) * scale + bias_c
    p_c = jax.nn.softmax(jnp.where(vis_c, s_c, NEG_INF), axis=2) * vis_c
    o_cmp = jnp.einsum('bqngr,bngd->bqgrd', p_c.astype(v_cmp.dtype), v_cmp)
    imp = jnp.swapaxes(p_c.sum(-1), 2, 3)
    lead = CMP_RATIO - 1
    total = SLC_RATIO * n_slc + lead
    imp = jnp.pad(imp, ((0, 0), (0, 0), (0, 0), (lead, total - lead - n_cmp)))
    span = SLC_RATIO * (n_slc - 1) + 1
    imp_slc = sum(imp[..., m - n + lead: m - n + lead + span: SLC_RATIO]
                  for m in range(SLC_RATIO) for n in range(CMP_RATIO))
    blk = jnp.arange(n_slc, dtype=jnp.int32)[None, :]
    cur = (pos // SLC_BLOCK)[:, None]
    vis_s = blk * SLC_BLOCK <= pos[:, None]
    forced = (blk == 0) | (blk == cur) | (blk == cur - 1)
    score = jnp.where(forced[None, :, None], FORCED_SCORE,
                      jnp.where(vis_s[None, :, None], imp_slc, -1.0))
    n_sel = min(N_SELECT, n_slc)
    _, idx = lax.top_k(score, n_sel)
    b_idx = jnp.arange(B)[:, None, None, None]
    g_idx = jnp.arange(G)[None, None, :, None]
    n_keys = n_sel * SLC_BLOCK
    ks = k_slc[b_idx, g_idx, idx].reshape(B, Q, G, n_keys, HEAD_DIM)
    vs = v_slc[b_idx, g_idx, idx].reshape(B, Q, G, n_keys, HEAD_DIM)
    key_pos = (idx[..., None] * SLC_BLOCK + jnp.arange(SLC_BLOCK, dtype=jnp.int32)).reshape(B, Q, G, n_keys)
    dist_s = pos[None, :, None, None] - key_pos
    table_g = rel_bias.reshape(N_BUCKETS, G, R).transpose(1, 0, 2)
    bias_s = table_g[g_idx, t5_bucket(dist_s)]
    s_s = jnp.einsum('bqgrd,bqgkd->bqgkr', q, ks).astype(jnp.float32) * scale + bias_s
    p_s = jax.nn.softmax(jnp.where((dist_s >= 0)[..., None], s_s, NEG_INF), axis=3)
    o_slc = jnp.einsum('bqgkr,bqgkd->bqgrd', p_s.astype(vs.dtype), vs)
    return o_cmp, o_slc


def nsa_global(q, pos, rows, cmp_pe, cmp_w1, cmp_w2, rel_bias, q_block):
    B, T = rows.shape[:2]
    k_cmp = compress_rows(rows[:, :, 0], cmp_pe[0], cmp_w1[0], cmp_w2[0])
    v_cmp = compress_rows(rows[:, :, 1], cmp_pe[1], cmp_w1[1], cmp_w2[1])
    n_slc = -(-T // SLC_BLOCK)
    slc = jnp.pad(rows[:, :, 2:4], ((0, 0), (0, n_slc * SLC_BLOCK - T), (0, 0), (0, 0), (0, 0)))
    slc = slc.reshape(B, n_slc, SLC_BLOCK, 2, NSA_KV_HEADS, HEAD_DIM).transpose(3, 0, 4, 1, 2, 5)
    Q = q.shape[1]
    nqb = Q // q_block
    qb = jnp.swapaxes(q.reshape(B, nqb, q_block, NSA_KV_HEADS, NSA_GROUP, HEAD_DIM), 0, 1)
    pb = pos.reshape(nqb, q_block)
    o_cmp, o_slc = lax.map(
        lambda a: nsa_query_block(a[0], a[1], k_cmp, v_cmp, slc[0], slc[1], rel_bias), (qb, pb))
    o_cmp = jnp.swapaxes(o_cmp, 0, 1).reshape(B, Q, NSA_KV_HEADS, NSA_GROUP, HEAD_DIM)
    o_slc = jnp.swapaxes(o_slc, 0, 1).reshape(B, Q, NSA_KV_HEADS, NSA_GROUP, HEAD_DIM)
    return o_cmp, o_slc


def window_attend(q, k, v, q_pos, k_pos, rel_bias):
    B, N, Q, G, R, _ = q.shape
    K = k.shape[2]
    dist = q_pos[:, :, None] - k_pos[:, None, :]
    valid = ((dist >= 0) & (dist < WINDOW) & (k_pos[:, None, :] >= 0))[None, :, :, :, None, None]
    bias = rel_bias[t5_bucket(dist)].reshape(N, Q, K, G, R)
    s = jnp.einsum('bnqgrd,bnkgd->bnqkgr', q, k).astype(jnp.float32) * (HEAD_DIM ** -0.5) + bias
    p = jax.nn.softmax(jnp.where(valid, s, NEG_INF), axis=3)
    return jnp.einsum('bnqkgr,bnkgd->bnqgrd', p.astype(v.dtype), v)


def gla_chunked(q, k, v, log_a, s0):
    B, T, H, _ = q.shape
    DV = v.shape[-1]
    C = min(GLA_CHUNK, T)
    nc = -(-T // C)
    pad = nc * C - T

    def blocks(a):
        a = jnp.pad(a.astype(jnp.float32), ((0, 0), (0, pad), (0, 0), (0, 0)))
        return jnp.swapaxes(a.reshape(B, nc, C, H, a.shape[-1]), 0, 1)

    causal = jnp.tril(jnp.ones((C, C), dtype=bool))

    def step(S, blk):
        qc, kc, vc, ac = blk
        b = jnp.cumsum(ac, axis=1)
        qe = qc * jnp.exp(b)
        ke = kc * jnp.exp(-b)
        att = jnp.where(causal, jnp.einsum('bihd,bjhd->bhij', qe, ke), 0.0)
        o = jnp.einsum('bhij,bjhv->bihv', att, vc) + jnp.einsum('bihd,bhdv->bihv', qe, S)
        b_last = b[:, -1]
        kd = kc * jnp.exp(b_last[:, None] - b)
        S = S * jnp.exp(b_last)[..., None] + jnp.einsum('bjhd,bjhv->bhdv', kd, vc)
        return S, o

    S, o = lax.scan(step, s0.astype(jnp.float32), (blocks(q), blocks(k), blocks(v), blocks(log_a)))
    o = jnp.swapaxes(o, 0, 1).reshape(B, nc * C, H, DV)[:, :T]
    return o, S


def gla_branch(q_g, k_g, v_g, lr, w_gk, b_gk, s0):
    B, T = q_g.shape[:2]
    log_a = jax.nn.log_sigmoid((lr @ w_gk + b_gk).astype(jnp.float32)) / GLA_NORMALIZER
    o, S = gla_chunked(q_g, k_g, v_g, log_a.reshape(B, T, GLA_HEADS, GLA_DK), s0)
    return o.astype(q_g.dtype), S


def mixer_output(o_cmp, o_slc, o_win, gates, o_gla, g_gla, gla_norm, w_o):
    B, T = o_cmp.shape[:2]
    o_nsa = (gates[:, :, 0] * o_cmp + gates[:, :, 1] * o_slc + gates[:, :, 2] * o_win).reshape(B, T, NSA_Q_DIM)
    o_g = (rmsnorm(o_gla, gla_norm) * jax.nn.silu(g_gla)).reshape(B, T, GLA_V_DIM)
    return jnp.concatenate([o_nsa, o_g], axis=-1) @ w_o


def mix_prompt(xn, w_in, cmp_pe, cmp_w1, cmp_w2, w_gk, b_gk, gla_norm, w_o, rel_bias):
    B, T, _ = xn.shape
    q_n, kv, gates, q_g, k_g, v_g, lr, g_g = split_projection(xn, w_in)
    pos = jnp.arange(T, dtype=jnp.int32)
    o_cmp, o_slc = nsa_global(q_n, pos, kv[:, :, :4], cmp_pe, cmp_w1, cmp_w2, rel_bias, min(Q_BLOCK, T))
    nb = T // Q_BLOCK
    nwb = WINDOW // Q_BLOCK
    wk = jnp.pad(kv[:, :, 4:], ((0, 0), (nwb * Q_BLOCK, 0), (0, 0), (0, 0), (0, 0)))
    wk = wk.reshape(B, nb + nwb, Q_BLOCK, 2, NSA_KV_HEADS, HEAD_DIM)
    band = jnp.concatenate([wk[:, j:j + nb] for j in range(nwb + 1)], axis=2)
    k_pos = (jnp.arange(nb, dtype=jnp.int32)[:, None] - nwb) * Q_BLOCK + jnp.arange((nwb + 1) * Q_BLOCK, dtype=jnp.int32)[None, :]
    o_win = window_attend(q_n.reshape(B, nb, Q_BLOCK, NSA_KV_HEADS, NSA_GROUP, HEAD_DIM),
                          band[:, :, :, 0], band[:, :, :, 1], pos.reshape(nb, Q_BLOCK), k_pos, rel_bias)
    o_win = o_win.reshape(B, T, NSA_KV_HEADS, NSA_GROUP, HEAD_DIM)
    s0 = jnp.zeros((B, GLA_HEADS, GLA_DK, GLA_DV), jnp.float32)
    o_gla, s_new = gla_branch(q_g, k_g, v_g, lr, w_gk, b_gk, s0)
    out = mixer_output(o_cmp, o_slc, o_win, gates, o_gla, g_g, gla_norm, w_o)
    keep = min(WINDOW, T)
    return out, kv[:, :, :4], kv[:, T - keep:, 4:], s_new


def mix_sample(xn, cache_kv, page_table, win_buf, s0, w_in, cmp_pe, cmp_w1, cmp_w2, w_gk, b_gk,
               gla_norm, w_o, rel_bias):
    B, Tn, _ = xn.shape
    q_n, kv, gates, q_g, k_g, v_g, lr, g_g = split_projection(xn, w_in)
    n_pages = page_table.shape[1]
    past_len = n_pages * cache_kv.shape[1]
    past = cache_kv[page_table].reshape(B, past_len, 4, NSA_KV_HEADS, HEAD_DIM)
    rows_full = jnp.concatenate([past, kv[:, :, :4].astype(past.dtype)], axis=1)
    pos = past_len + jnp.arange(Tn, dtype=jnp.int32)
    o_cmp, o_slc = nsa_global(q_n, pos, rows_full, cmp_pe, cmp_w1, cmp_w2, rel_bias, Tn)
    w_buf = win_buf.shape[1]
    win_full = jnp.concatenate([win_buf, kv[:, :, 4:].astype(win_buf.dtype)], axis=1)
    k_pos = past_len - w_buf + jnp.arange(w_buf + Tn, dtype=jnp.int32)
    o_win = window_attend(q_n[:, None], win_full[:, None, :, 0], win_full[:, None, :, 1],
                          pos[None], k_pos[None], rel_bias)[:, 0]
    o_gla, s_new = gla_branch(q_g, k_g, v_g, lr, w_gk, b_gk, s0)
    out = mixer_output(o_cmp, o_slc, o_win, gates, o_gla, g_g, gla_norm, w_o)
    return out, kv[:, :, :4], win_full[:, Tn:], s_new


def ffn_ple(h, p_l, norm_ffn, w_gate, w_up, w_down, w_ple, norm_ple, w_ple_gate):
    xn = rmsnorm(h, norm_ffn)
    h = h + (jax.nn.silu(xn @ w_gate) * (xn @ w_up)) @ w_down
    gate = jax.nn.sigmoid(rmsnorm(h, norm_ple) @ w_ple_gate)
    return h + (p_l @ w_ple) * gate


def setup_inputs(seed: int = 0) -> dict:
    key = jax.random.key(seed)
    ks = iter(jax.random.split(key, 32))
    nrm = lambda shape, s=1.0: jax.random.normal(next(ks), shape, jnp.float32) * s
    n_pages = PAST_LEN // PAGE_SIZE
    used = DEC_BATCH * n_pages
    n_pool = used + max(1, used // 4)
    w_buf = min(WINDOW, PAST_LEN)
    page_table = jax.random.permutation(next(ks), n_pool)[:used].reshape(DEC_BATCH, n_pages).astype(jnp.int32)
    return {
        'x_prompt': nrm((BATCH, SEQ, D_MODEL)),
        'x_sample': nrm((DEC_BATCH, DEC_SEQ, D_MODEL)),
        'cache_nsa_kv': nrm((DEPTH, n_pool, PAGE_SIZE, 4, NSA_KV_HEADS, HEAD_DIM)),
        'cache_win_kv': nrm((DEPTH, DEC_BATCH, w_buf, 2, NSA_KV_HEADS, HEAD_DIM)),
        'state_gla': nrm((DEPTH, DEC_BATCH, GLA_HEADS, GLA_DK, GLA_DV)),
        'page_table': page_table,
        'p_prompt': nrm((DEPTH, BATCH, SEQ, D_PLE)),
        'p_sample': nrm((DEPTH, DEC_BATCH, DEC_SEQ, D_PLE)),
        'norm_mix': 1.0 + nrm((DEPTH, D_MODEL), 0.1),
        'w_in': nrm((DEPTH, D_MODEL, IN_DIM), D_MODEL ** -0.5),
        'cmp_pe': nrm((DEPTH, 2, CMP_LEN, HEAD_DIM), 0.1),
        'cmp_w1': nrm((DEPTH, 2, CMP_LEN * HEAD_DIM, CMP_HIDDEN), (CMP_LEN * HEAD_DIM) ** -0.5),
        'cmp_w2': nrm((DEPTH, 2, CMP_HIDDEN, HEAD_DIM), CMP_HIDDEN ** -0.5),
        'w_gk': nrm((DEPTH, GLA_LOWRANK, GLA_QK_DIM), GLA_LOWRANK ** -0.5),
        'b_gk': nrm((DEPTH, GLA_QK_DIM), 0.1),
        'gla_norm': 1.0 + nrm((DEPTH, GLA_DV), 0.1),
        'w_o': nrm((DEPTH, MIX_WIDTH, D_MODEL), MIX_WIDTH ** -0.5),
        'norm_ffn': 1.0 + nrm((DEPTH, D_MODEL), 0.1),
        'w_gate': nrm((DEPTH, D_MODEL, D_FF), D_MODEL ** -0.5),
        'w_up': nrm((DEPTH, D_MODEL, D_FF), D_MODEL ** -0.5),
        'w_down': nrm((DEPTH, D_FF, D_MODEL), D_FF ** -0.5),
        'w_ple': nrm((DEPTH, D_PLE, D_MODEL), D_PLE ** -0.5),
        'norm_ple': 1.0 + nrm((DEPTH, D_MODEL), 0.1),
        'w_ple_gate': nrm((DEPTH, D_MODEL, D_MODEL), D_MODEL ** -0.5),
        'rel_bias': nrm((N_BUCKETS, NSA_HEADS), 0.5),
        'norm_final': 1.0 + nrm((D_MODEL,), 0.1),
    }


def reference(x_prompt, x_sample, cache_nsa_kv, cache_win_kv, state_gla, page_table, p_prompt, p_sample,
              norm_mix, w_in, cmp_pe, cmp_w1, cmp_w2, w_gk, b_gk, gla_norm, w_o, norm_ffn, w_gate, w_up,
              w_down, w_ple, norm_ple, w_ple_gate, rel_bias, norm_final):
    kv_p, kv_s, win_p, win_s, st_p, st_s = [], [], [], [], [], []
    h_p, h_s = x_prompt, x_sample
    for l in range(DEPTH):
        lw = (w_in[l], cmp_pe[l], cmp_w1[l], cmp_w2[l], w_gk[l], b_gk[l], gla_norm[l], w_o[l], rel_bias)
        fw = (norm_ffn[l], w_gate[l], w_up[l], w_down[l], w_ple[l], norm_ple[l], w_ple_gate[l])
        mix, rows, win, st = mix_prompt(rmsnorm(h_p, norm_mix[l]), *lw)
        h_p = ffn_ple(h_p + mix, p_prompt[l], *fw)
        kv_p.append(rows)
        win_p.append(win)
        st_p.append(st)
        mix, rows, win, st = mix_sample(rmsnorm(h_s, norm_mix[l]), cache_nsa_kv[l], page_table,
                                        cache_win_kv[l], state_gla[l], *lw)
        h_s = ffn_ple(h_s + mix, p_sample[l], *fw)
        kv_s.append(rows)
        win_s.append(win)
        st_s.append(st)
    y_prompt = rmsnorm(h_p, norm_final)
    y_sample = rmsnorm(h_s, norm_final)
    new_kv_prompt = jnp.stack(kv_p)
    new_kv_sample = jnp.stack(kv_s)
    new_win_prompt = jnp.stack(win_p)
    new_win_sample = jnp.stack(win_s)
    new_gla_prompt = jnp.stack(st_p)
    new_gla_sample = jnp.stack(st_s)
    return (y_prompt, y_sample, new_kv_prompt, new_kv_sample, new_win_prompt, new_win_sample, new_gla_prompt, new_gla_sample)
```

```python
import functools
import math

import jax
import jax.numpy as jnp
import numpy as np
from jax import lax
from jax.experimental import pallas as pl
from jax.experimental.pallas import tpu as pltpu

f32 = jnp.float32
bf16 = jnp.bfloat16

D_MODEL = 1024
BATCH = 4
SEQ = 4096
DEC_BATCH = 32
DEC_SEQ = 4
PAST_LEN = 8192
PAGE_SIZE = 128
HEAD_DIM = 64
NSA_HEADS = 8
NSA_KV_HEADS = 2
NSA_GROUP = 4
CMP_LEN = 32
CMP_STRIDE = 16
CMP_RATIO = 2
CMP_HIDDEN = 64
SLC_BLOCK = 64
SLC_RATIO = 4
N_SELECT = 16
WINDOW = 512
GLA_HEADS = 4
GLA_DK = 64
GLA_DV = 128
GLA_LOWRANK = 16
GLA_NORMALIZER = 16.0
GLA_CHUNK = 32
N_BUCKETS = 32
MAX_DISTANCE = 128
D_FF = 2816
D_PLE = 256
EPS = 1e-6
NEG_INF = -1e30
FORCED_SCORE = 1e9

C_QN = (0, 512)
C_KVM = (512, 1024)
C_KVW = (1024, 1280)
C_QG = (1280, 1536)
C_KG = (1536, 1792)
C_VG = (1792, 2304)
C_GG = (2304, 2816)
C_MISC = (2816, 2944)
PACKED_DIM = 2944
LR_LANE = 32

TQ = 128
N_CMP_P = SEQ // CMP_STRIDE
N_SLC_P = SEQ // SLC_BLOCK
N_SUB_S = PAST_LEN // CMP_STRIDE
N_SLC_S = -(-(PAST_LEN + DEC_SEQ) // SLC_BLOCK)
NBP_S = 144
KEYS_S = PAST_LEN + 128
WBAND_P = 640
SBAND_P = 512
WBAND_S = 640
N_PAGES = PAST_LEN // PAGE_SIZE

VMEM_LIMIT = 58 * 1024 * 1024

_NT = (((1,), (1,)), ((), ()))
_TN = (((0,), (0,)), ((), ()))


def _dot(a, b):
    return jnp.dot(a, b, preferred_element_type=f32)


def _dot_nt(a, b):
    return lax.dot_general(a, b, _NT, preferred_element_type=f32)


def _dot_tn(a, b):
    return lax.dot_general(a, b, _TN, preferred_element_type=f32)


def _split3(x):
    hi = x.astype(bf16)
    r1 = x - hi.astype(f32)
    mid = r1.astype(bf16)
    lo = (r1 - mid.astype(f32)).astype(bf16)
    return hi, mid, lo


def _const_spec(shape):
    n = len(shape)
    return pl.BlockSpec(shape, lambda *_: (0,) * n, pipeline_mode=pl.Buffered(1))


def _proj_kernel(emit_kt, x_ref, g_ref, w_ref, wgk_ref, bgk_ref, qn_ref, kvm_ref, kvw_ref, misc_ref,
                 qg_ref, kg_ref, vg_ref, gg_ref, loga_ref, *kt_refs):
    x = x_ref[...]
    xn = (x * lax.rsqrt(jnp.mean(x * x, axis=-1, keepdims=True) + EPS)) * g_ref[...]
    xb = xn.astype(bf16)

    def seg(c):
        return _dot(xb, w_ref[:, c[0]:c[1]])

    qn_ref[...] = (seg(C_QN) * (HEAD_DIM ** -0.5)).astype(bf16)
    kvm = seg(C_KVM)
    kvm_ref[...] = kvm
    kvw = seg(C_KVW)
    kvw_ref[...] = kvw
    qg_ref[...] = seg(C_QG) * (GLA_DK ** -0.5)
    kg_ref[...] = seg(C_KG)
    vg_ref[...] = seg(C_VG).astype(bf16)
    gg_ref[...] = seg(C_GG)
    misc = seg(C_MISC)
    misc_ref[...] = misc
    z = _dot(misc.astype(bf16), wgk_ref[...]) + bgk_ref[...]
    loga_ref[...] = jax.nn.log_sigmoid(z) / GLA_NORMALIZER
    if emit_kt:
        kt_ref, vb_ref = kt_refs
        kt_ref[0, 0] = kvm[:, 256:384].T.astype(bf16)
        kt_ref[0, 1] = kvw[:, 0:128].T.astype(bf16)
        vb_ref[0, :, 0:128] = kvm[:, 384:512].astype(bf16)
        vb_ref[0, :, 128:256] = kvw[:, 128:256].astype(bf16)


def _project(x2d, g, w_packed, wgk_pad, bgk, *, tm, seq_per_batch=None):
    m = x2d.shape[0]
    emit_kt = seq_per_batch is not None
    row = lambda w: pl.BlockSpec((tm, w), lambda i: (i, 0))
    out_shape = [
        jax.ShapeDtypeStruct((m, 512), bf16),
        jax.ShapeDtypeStruct((m, 512), f32),
        jax.ShapeDtypeStruct((m, 256), f32),
        jax.ShapeDtypeStruct((m, 128), f32),
        jax.ShapeDtypeStruct((m, 256), f32),
        jax.ShapeDtypeStruct((m, 256), f32),
        jax.ShapeDtypeStruct((m, 512), bf16),
        jax.ShapeDtypeStruct((m, 512), f32),
        jax.ShapeDtypeStruct((m, 256), f32),
    ]
    out_specs = [row(512), row(512), row(256), row(128), row(256), row(256), row(512), row(512), row(256)]
    if emit_kt:
        nt = seq_per_batch // tm
        nb = m // seq_per_batch
        out_shape += [jax.ShapeDtypeStruct((nb, 2, 128, seq_per_batch), bf16),
                      jax.ShapeDtypeStruct((nb, seq_per_batch, 256), bf16)]
        out_specs += [pl.BlockSpec((1, 2, 128, tm), lambda i: (i // nt, 0, 0, i % nt)),
                      pl.BlockSpec((1, tm, 256), lambda i: (i // nt, i % nt, 0))]
    return pl.pallas_call(
        functools.partial(_proj_kernel, emit_kt),
        grid=(m // tm,),
        in_specs=[row(D_MODEL), _const_spec((1, D_MODEL)), _const_spec((D_MODEL, PACKED_DIM)),
                  _const_spec((128, 256)), _const_spec((1, 256))],
        out_specs=out_specs,
        out_shape=out_shape,
        compiler_params=pltpu.CompilerParams(dimension_semantics=("arbitrary",), vmem_limit_bytes=VMEM_LIMIT),
        name="proj",
    )(x2d, g, w_packed, wgk_pad, bgk)


def _compress_rows(x_ref, pe_ref, w1_ref, w2_ref):
    nb = x_ref.shape[0]
    h0 = jnp.zeros((nb, 256), f32)
    h1 = jnp.zeros((nb, 256), f32)
    for s in range(CMP_STRIDE):
        xs = x_ref[:, s, :]
        h0 = h0 + _dot((xs + pe_ref[0, s:s + 1, :]).astype(bf16), w1_ref[0, s])
        h1 = h1 + _dot((xs + pe_ref[1, s:s + 1, :]).astype(bf16), w1_ref[1, s])
    hid = h0 + pltpu.roll(h1, nb - 1, 0)
    return _dot(jax.nn.gelu(hid).astype(bf16), w2_ref[...])


def _compress_kernel(x_ref, pe_ref, w1_ref, w2_ref, o_ref):
    o_ref[0] = _compress_rows(x_ref.at[0], pe_ref, w1_ref, w2_ref)


def _compress_prompt(kvm, pe, w1bd, w2bd):
    x4 = kvm.reshape(BATCH, N_CMP_P, CMP_STRIDE, 512)
    return pl.pallas_call(
        _compress_kernel,
        grid=(BATCH,),
        in_specs=[pl.BlockSpec((1, N_CMP_P, CMP_STRIDE, 256), lambda b: (b, 0, 0, 0)),
                  _const_spec((2, CMP_STRIDE, 256)), _const_spec((2, CMP_STRIDE, 256, 256)),
                  _const_spec((256, 256))],
        out_specs=pl.BlockSpec((1, N_CMP_P, 256), lambda b: (b, 0, 0)),
        out_shape=jax.ShapeDtypeStruct((BATCH, N_CMP_P, 256), f32),
        compiler_params=pltpu.CompilerParams(dimension_semantics=("arbitrary",), vmem_limit_bytes=VMEM_LIMIT),
        name="compress",
    )(x4, pe, w1bd, w2bd)


def _masked_softmax_rows(s, bias):
    s = s + bias
    vis = bias > 0.5 * NEG_INF
    m = jnp.max(s, axis=-1, keepdims=True)
    p = jnp.where(vis, jnp.exp(s - m), 0.0)
    l = jnp.sum(p, axis=-1, keepdims=True)
    return p * (1.0 / jnp.where(l > 0.0, l, 1.0))


def _select_blocks(score_ref, rank_rows, n_blocks):
    score = score_ref[...]
    blk = lax.broadcasted_iota(jnp.int32, score.shape, 0)

    def body(k, rank):
        rk = score_ref[pl.ds(k, 1), :]
        ge = jnp.where(rk >= score, 1.0, 0.0)
        gt = jnp.where(rk > score, 1.0, 0.0)
        return rank + jnp.where(blk > k, ge, gt)

    rank = lax.fori_loop(0, n_blocks, body, jnp.zeros(score.shape, f32))
    return jnp.where(rank < float(N_SELECT), 1.0, 0.0)


def _selection_scores(isl, pos, n_blocks):
    blk = lax.broadcasted_iota(jnp.int32, isl.shape, 0)
    cur = lax.shift_right_logical(pos, 6)
    forced = jnp.where(blk == 0, 1.0, 0.0) + jnp.where(blk == cur, 1.0, 0.0) + jnp.where(blk == cur - 1, 1.0, 0.0)
    vis = blk * SLC_BLOCK <= pos
    score = jnp.where(forced > 0.5, FORCED_SCORE, jnp.where(vis, isl, -1.0))
    return jnp.where(blk < n_blocks, score, -2.0)


def _block_expand(rows, n_keys, first_block):
    r = lax.broadcasted_iota(jnp.int32, (rows, n_keys), 0)
    c = lax.shift_right_logical(lax.broadcasted_iota(jnp.int32, (rows, n_keys), 1), 6) + first_block
    return jnp.where(r == c, 1.0, 0.0).astype(bf16)


def _nsa_prompt_kernel(q_ref, cmp_ref, kst_ref, kwt_ref, vs_ref, vw_ref, misc_ref, bc_ref, wt_ref, st_ref,
                       at_ref, o_ref, score_ref):
    t = pl.program_id(1)
    q0 = t * TQ
    q = q_ref[0]
    gates = jax.nn.sigmoid(misc_ref[0])
    cmpv = cmp_ref[0]
    vc = cmpv[:, 128:256].astype(bf16)
    n_far = jnp.maximum(t // 2 - 1, 0)
    far_end = pl.multiple_of(n_far * 256, 256)
    w_start = pl.multiple_of(jnp.maximum(t - 4, 0) * TQ, TQ)
    pos = q0 + lax.broadcasted_iota(jnp.int32, (N_SLC_P, TQ), 1)
    pieces = []
    for g in range(NSA_KV_HEADS):
        gl = slice(g * HEAD_DIM, (g + 1) * HEAD_DIM)
        q4 = jnp.concatenate([q[:, g * 256 + r * 64: g * 256 + (r + 1) * 64] for r in range(NSA_GROUP)], axis=0)

        kc = cmpv[:, gl].astype(bf16)
        pc = _masked_softmax_rows(_dot_nt(q4, kc), bc_ref[g * 4:(g + 1) * 4].reshape(4 * TQ, N_CMP_P))
        o_cmp = _dot(pc.astype(bf16), vc)[:, gl]

        imp = pc[0:TQ] + pc[TQ:2 * TQ] + pc[2 * TQ:3 * TQ] + pc[3 * TQ:4 * TQ]
        isl = sum(_dot_nt(at_ref[...], part) for part in _split3(imp))
        score_ref[...] = _selection_scores(isl, pos, N_SLC_P)
        sel_t = _select_blocks(score_ref, N_SLC_P, N_SLC_P)
        sel = jnp.concatenate([sel_t, jnp.zeros((128 - N_SLC_P, TQ), f32)], axis=0).T.astype(bf16)

        def far_body(j, carry):
            m, l, acc = carry
            k0 = pl.multiple_of(j * 256, 256)
            s = _dot(q4, kst_ref[0, g, :, pl.ds(k0, 256)])
            neg = (_dot(sel, _block_expand(128, 256, 4 * j)) - 1.0) * (-NEG_INF)
            s = (s.reshape(4, TQ, 256) + neg[None]).reshape(4 * TQ, 256)
            m_new = jnp.maximum(m, jnp.max(s, axis=-1, keepdims=True))
            alpha = jnp.exp(m - m_new)
            p = jnp.exp(s - m_new)
            l = alpha * l + jnp.sum(p, axis=-1, keepdims=True)
            acc = alpha * acc + _dot(p.astype(bf16), vs_ref[0, pl.ds(k0, 256), :])
            return m_new, l, acc

        init = (jnp.full((4 * TQ, 1), NEG_INF, f32), jnp.zeros((4 * TQ, 1), f32), jnp.zeros((4 * TQ, 128), f32))
        m, l, acc = lax.fori_loop(0, n_far, far_body, init)
        s = _dot(q4, kst_ref[0, g, :, pl.ds(far_end, SBAND_P)])
        s = s + st_ref[0, g * 4:(g + 1) * 4].reshape(4 * TQ, SBAND_P)
        neg = (_dot(sel, _block_expand(128, SBAND_P, 4 * n_far)) - 1.0) * (-NEG_INF)
        s = (s.reshape(4, TQ, SBAND_P) + neg[None]).reshape(4 * TQ, SBAND_P)
        m_new = jnp.maximum(m, jnp.max(s, axis=-1, keepdims=True))
        alpha = jnp.exp(m - m_new)
        p = jnp.exp(s - m_new)
        l = alpha * l + jnp.sum(p, axis=-1, keepdims=True)
        acc = alpha * acc + _dot(p.astype(bf16), vs_ref[0, pl.ds(far_end, SBAND_P), :])
        o_slc = (acc * (1.0 / l))[:, gl]

        s = _dot(q4, kwt_ref[0, g, :, pl.ds(w_start, WBAND_P)])
        s = s + wt_ref[0, g * 4:(g + 1) * 4].reshape(4 * TQ, WBAND_P)
        m = jnp.max(s, axis=-1, keepdims=True)
        p = jnp.exp(s - m)
        l = jnp.sum(p, axis=-1, keepdims=True)
        o_win = (_dot(p.astype(bf16), vw_ref[0, pl.ds(w_start, WBAND_P), :]) * (1.0 / l))[:, gl]

        for r in range(NSA_GROUP):
            rows = slice(r * TQ, (r + 1) * TQ)
            c = g * NSA_GROUP + r
            pieces.append(gates[:, c:c + 1] * o_cmp[rows] + gates[:, 8 + c:9 + c] * o_slc[rows]
                          + gates[:, 16 + c:17 + c] * o_win[rows])
    o_ref[0] = jnp.concatenate(pieces, axis=1).astype(bf16)


def _nsa_prompt(qn, cmp, kt, vb, misc, bias_c, wtab, stab, at):
    nq = SEQ // TQ
    kt4 = kt.reshape(BATCH, 4, HEAD_DIM, SEQ)
    return pl.pallas_call(
        _nsa_prompt_kernel,
        grid=(BATCH, nq),
        in_specs=[
            pl.BlockSpec((1, TQ, 512), lambda b, t: (b, t, 0)),
            pl.BlockSpec((1, N_CMP_P, 256), lambda b, t: (b, 0, 0)),
            pl.BlockSpec((1, 2, HEAD_DIM, SEQ), lambda b, t: (b, 0, 0, 0)),
            pl.BlockSpec((1, 2, HEAD_DIM, SEQ), lambda b, t: (b, 1, 0, 0)),
            pl.BlockSpec((1, SEQ, 128), lambda b, t: (b, 0, 0)),
            pl.BlockSpec((1, SEQ, 128), lambda b, t: (b, 0, 1)),
            pl.BlockSpec((1, TQ, 128), lambda b, t: (b, t, 0)),
            pl.BlockSpec((NSA_HEADS, TQ, N_CMP_P), lambda b, t: (0, t, 0)),
            pl.BlockSpec((1, NSA_HEADS, TQ, WBAND_P), lambda b, t: (jnp.minimum(t, 4), 0, 0, 0)),
            pl.BlockSpec((1, NSA_HEADS, TQ, SBAND_P), lambda b, t: (jnp.where(t < 2, t, 2 + t % 2), 0, 0, 0)),
            _const_spec((N_SLC_P, N_CMP_P)),
        ],
        out_specs=pl.BlockSpec((1, TQ, 512), lambda b, t: (b, t, 0)),
        out_shape=jax.ShapeDtypeStruct((BATCH, SEQ, 512), bf16),
        scratch_shapes=[pltpu.VMEM((N_SLC_P, TQ), f32)],
        compiler_params=pltpu.CompilerParams(dimension_semantics=("arbitrary", "arbitrary"),
                                             vmem_limit_bytes=VMEM_LIMIT),
        name="nsa_prompt",
    )(qn, cmp, kt4, kt4, vb, vb, misc, bias_c, wtab, stab, at)


def _nsa_sample_kernel(pt_ref, q_ref, gate_ref, kvn_ref, kwn_ref, win_ref, cache_c, cache_s, pe_ref, w1_ref,
                       w2_ref, at_ref, ex_ref, bc_ref, bs_ref, bw_ref, o_ref, xc, xs, wb, score_ref, sems):
    b = pl.program_id(0)
    nb = pl.num_programs(0)
    slot = lax.rem(b, 2)

    def page_copies(seq, sl, p):
        page = pt_ref[seq, p]
        c_cp = pltpu.make_async_copy(cache_c.at[page, :, :, pl.ds(0, 256)], xc.at[sl, pl.ds(p * 8, 8)],
                                     sems.at[sl, 0])
        s_cp = pltpu.make_async_copy(cache_s.at[page, :, pl.ds(256, 256)], xs.at[sl, pl.ds(p * PAGE_SIZE, PAGE_SIZE)],
                                     sems.at[sl, 1])
        return c_cp, s_cp

    def start_all(seq, sl):
        for p in range(N_PAGES):
            c_cp, s_cp = page_copies(seq, sl, p)
            c_cp.start()
            s_cp.start()

    @pl.when(b == 0)
    def _():
        start_all(0, 0)

    @pl.when(b + 1 < nb)
    def _():
        start_all(b + 1, 1 - slot)

    xs[slot, pl.ds(PAST_LEN, 128), :] = jnp.zeros((128, 256), f32)
    xs[slot, pl.ds(PAST_LEN, DEC_SEQ), :] = kvn_ref[0][:, 256:512]
    wb[0:WINDOW, :] = win_ref[0]
    wb[WINDOW:WBAND_S, :] = jnp.zeros((WBAND_S - WINDOW, 256), f32)
    wb[WINDOW:WINDOW + DEC_SEQ, :] = kwn_ref[0]

    for p in range(N_PAGES):
        c_cp, s_cp = page_copies(b, slot, p)
        c_cp.wait()
        s_cp.wait()

    cmpv = _compress_rows(xc.at[slot], pe_ref, w1_ref, w2_ref)
    kc = cmpv[:, 0:128].astype(bf16)
    vc = cmpv[:, 128:256].astype(bf16)
    o_cmp, imps = [], []
    for g in range(NSA_KV_HEADS):
        pc = _masked_softmax_rows(_dot_nt(q_ref[0, g], kc), bc_ref[g])
        o_cmp.append(_dot(pc.astype(bf16), vc)[:, g * HEAD_DIM:(g + 1) * HEAD_DIM])
        imps.append(pc + pltpu.roll(pc, 4, 0) + pltpu.roll(pc, 8, 0) + pltpu.roll(pc, 12, 0))
    imp = jnp.concatenate(imps + [jnp.zeros((128 - 32, N_SUB_S), f32)], axis=0)
    isl = sum(_dot_nt(at_ref[...], part) for part in _split3(imp))
    lane = lax.broadcasted_iota(jnp.int32, (NBP_S, 128), 1)
    pos = PAST_LEN + lax.bitwise_and(lane, DEC_SEQ - 1)
    score_ref[...] = _selection_scores(isl, pos, N_SLC_S)
    sel_t = _select_blocks(score_ref, NBP_S, N_SLC_S)
    sel = jnp.concatenate([sel_t, jnp.zeros((256 - NBP_S, 128), f32)], axis=0).T.astype(bf16)

    ks = xs[slot, :, 0:128].astype(bf16)
    vs = xs[slot, :, 128:256].astype(bf16)
    kw = wb[:, 0:128].astype(bf16)
    vw = wb[:, 128:256].astype(bf16)
    for g in range(NSA_KV_HEADS):
        gl = slice(g * HEAD_DIM, (g + 1) * HEAD_DIM)
        qg = q_ref[0, g]
        neg = (_dot(sel[g * 16:(g + 1) * 16], ex_ref[...]) - 1.0) * (-NEG_INF)
        s = _dot_nt(qg, ks) + bs_ref[g] + neg
        m = jnp.max(s, axis=-1, keepdims=True)
        p = jnp.exp(s - m)
        l = jnp.sum(p, axis=-1, keepdims=True)
        o_slc = (_dot(p.astype(bf16), vs) * (1.0 / l))[:, gl]
        s = _dot_nt(qg, kw) + bw_ref[g]
        m = jnp.max(s, axis=-1, keepdims=True)
        p = jnp.exp(s - m)
        l = jnp.sum(p, axis=-1, keepdims=True)
        o_win = (_dot(p.astype(bf16), vw) * (1.0 / l))[:, gl]
        gt = jax.nn.sigmoid(gate_ref[0, g])
        o_ref[0, g] = gt[:, 0:1] * o_cmp[g] + gt[:, 1:2] * o_slc + gt[:, 2:3] * o_win


def _nsa_sample(page_table, q4pad, gates, kvm_new, kvw_new, win_buf, cache, pe, w1bd, w2bd, at_s, expand_s,
                bc_s, bs_s, bw_s):
    n_pool = cache.shape[0]
    cache_c = cache.reshape(n_pool, PAGE_SIZE // CMP_STRIDE, CMP_STRIDE, 512)
    cache_s = cache.reshape(n_pool, PAGE_SIZE, 512)
    seq_spec = lambda shape: pl.BlockSpec((1,) + shape, lambda b, pt: (b,) + (0,) * len(shape))
    const = lambda shape: pl.BlockSpec(shape, lambda b, pt: (0,) * len(shape), pipeline_mode=pl.Buffered(1))
    grid_spec = pltpu.PrefetchScalarGridSpec(
        num_scalar_prefetch=1,
        grid=(DEC_BATCH,),
        in_specs=[
            seq_spec((2, 16, 128)), seq_spec((2, 16, 128)), seq_spec((DEC_SEQ, 512)), seq_spec((DEC_SEQ, 256)),
            seq_spec((WINDOW, 256)),
            pl.BlockSpec(memory_space=pl.ANY), pl.BlockSpec(memory_space=pl.ANY),
            const((2, CMP_STRIDE, 256)), const((2, CMP_STRIDE, 256, 256)), const((256, 256)),
            const((NBP_S, N_SUB_S)), const((256, KEYS_S)),
            const((2, 16, N_SUB_S)), const((2, 16, KEYS_S)), const((2, 16, WBAND_S)),
        ],
        out_specs=seq_spec((2, 16, HEAD_DIM)),
        scratch_shapes=[
            pltpu.VMEM((2, N_SUB_S, CMP_STRIDE, 256), f32),
            pltpu.VMEM((2, KEYS_S, 256), f32),
            pltpu.VMEM((WBAND_S, 256), f32),
            pltpu.VMEM((NBP_S, 128), f32),
            pltpu.SemaphoreType.DMA((2, 2)),
        ],
    )
    return pl.pallas_call(
        _nsa_sample_kernel,
        grid_spec=grid_spec,
        out_shape=jax.ShapeDtypeStruct((DEC_BATCH, 2, 16, HEAD_DIM), f32),
        compiler_params=pltpu.CompilerParams(dimension_semantics=("arbitrary",), vmem_limit_bytes=VMEM_LIMIT),
        name="nsa_sample",
    )(page_table, q4pad, gates, kvm_new, kvw_new, win_buf, cache_c, cache_s, pe, w1bd, w2bd, at_s, expand_s,
      bc_s, bs_s, bw_s)


def _gla_kernel(chunk, q_ref, k_ref, v_ref, gg_ref, a_ref, s0_ref, gn_ref, o_ref, sout_ref, s_ref):
    tt = pl.program_id(1)
    tb = q_ref.shape[1]

    @pl.when(tt == 0)
    def _():
        s_ref[...] = s0_ref[0]

    ri = lax.broadcasted_iota(jnp.int32, (chunk, chunk), 0)
    ci = lax.broadcasted_iota(jnp.int32, (chunk, chunk), 1)
    causal = ri >= ci
    tril = jnp.where(causal, 1.0, 0.0).astype(bf16)
    gn = gn_ref[...]

    def body(c, carry):
        r0 = pl.multiple_of(c * chunk, chunk)
        rows = pl.ds(r0, chunk)
        a = a_ref[0, rows, :]
        bcum = sum(_dot(tril, part) for part in _split3(a))
        b_last = bcum[chunk - 1:chunk, :]
        dec_col = jnp.exp(bcum.T[:, chunk - 1:chunk])
        q = q_ref[0, rows, :]
        k = k_ref[0, rows, :]
        qe = (q * jnp.exp(bcum)).astype(bf16)
        ke = (k * jnp.exp(-bcum)).astype(bf16)
        kd = (k * jnp.exp(b_last - bcum)).astype(bf16)
        for h in range(GLA_HEADS):
            hk = slice(h * GLA_DK, (h + 1) * GLA_DK)
            hv = slice(h * GLA_DV, (h + 1) * GLA_DV)
            v = v_ref[0, rows, hv]
            st = s_ref[h]
            att = jnp.where(causal, _dot_nt(qe[:, hk], ke[:, hk]), 0.0)
            o = _dot(att.astype(bf16), v) + _dot(qe[:, hk], st.astype(bf16))
            s_ref[h] = st * dec_col[hk, :] + _dot_tn(kd[:, hk], v)
            on = o * lax.rsqrt(jnp.mean(o * o, axis=-1, keepdims=True) + EPS) * gn
            o_ref[0, rows, hv] = (on * jax.nn.silu(gg_ref[0, rows, hv])).astype(bf16)
        return carry

    lax.fori_loop(0, tb // chunk, body, 0)

    @pl.when(tt == pl.num_programs(1) - 1)
    def _():
        sout_ref[0] = s_ref[...]


def _gla(qg, kg, vg, gg, loga, s0, gn, *, chunk, tb):
    nb, t, _ = qg.shape
    blk = lambda w: pl.BlockSpec((1, tb, w), lambda b, i: (b, i, 0))
    st_spec = pl.BlockSpec((1, GLA_HEADS, GLA_DK, GLA_DV), lambda b, i: (b, 0, 0, 0))
    return pl.pallas_call(
        functools.partial(_gla_kernel, chunk),
        grid=(nb, t // tb),
        in_specs=[blk(256), blk(256), blk(512), blk(512), blk(256), st_spec,
                  pl.BlockSpec((1, GLA_DV), lambda b, i: (0, 0))],
        out_specs=[blk(512), st_spec],
        out_shape=[jax.ShapeDtypeStruct((nb, t, 512), bf16),
                   jax.ShapeDtypeStruct((nb, GLA_HEADS, GLA_DK, GLA_DV), f32)],
        scratch_shapes=[pltpu.VMEM((GLA_HEADS, GLA_DK, GLA_DV), f32)],
        compiler_params=pltpu.CompilerParams(dimension_semantics=("arbitrary", "arbitrary"),
                                             vmem_limit_bytes=VMEM_LIMIT),
        name="gla",
    )(qg, kg, vg, gg, loga, s0, gn)


def _rms(x, g):
    return (x * lax.rsqrt(jnp.mean(x * x, axis=-1, keepdims=True) + EPS)) * g


def _post_kernel(x_ref, on_ref, og_ref, p_ref, wo_ref, nf_ref, wg_ref, wu_ref, wd_ref, wple_ref, npl_ref,
                 wpg_ref, nfin_ref, y_ref):
    h = x_ref[...] + _dot(on_ref[...], wo_ref[0:512, :]) + _dot(og_ref[...], wo_ref[512:1024, :])
    xn = _rms(h, nf_ref[...]).astype(bf16)
    ffn = None
    for cols in (slice(0, 1536), slice(1536, D_FF)):
        act = (jax.nn.silu(_dot(xn, wg_ref[:, cols])) * _dot(xn, wu_ref[:, cols])).astype(bf16)
        part = _dot(act, wd_ref[cols, :])
        ffn = part if ffn is None else ffn + part
    h = h + ffn
    gate = jax.nn.sigmoid(_dot(_rms(h, npl_ref[...]).astype(bf16), wpg_ref[...]))
    h = h + _dot(p_ref[...].astype(bf16), wple_ref[...]) * gate
    y_ref[...] = _rms(h, nfin_ref[...])


def _post(x2d, onsa, og, p2d, wo, nf, wg, wu, wd, wple, npl, wpg, nfin, *, tm):
    m = x2d.shape[0]
    row = lambda w: pl.BlockSpec((tm, w), lambda i: (i, 0))
    return pl.pallas_call(
        _post_kernel,
        grid=(m // tm,),
        in_specs=[row(D_MODEL), row(512), row(512), row(D_PLE),
                  _const_spec((D_MODEL, D_MODEL)), _const_spec((1, D_MODEL)),
                  _const_spec((D_MODEL, D_FF)), _const_spec((D_MODEL, D_FF)), _const_spec((D_FF, D_MODEL)),
                  _const_spec((D_PLE, D_MODEL)), _const_spec((1, D_MODEL)), _const_spec((D_MODEL, D_MODEL)),
                  _const_spec((1, D_MODEL))],
        out_specs=row(D_MODEL),
        out_shape=jax.ShapeDtypeStruct((m, D_MODEL), f32),
        compiler_params=pltpu.CompilerParams(dimension_semantics=("arbitrary",), vmem_limit_bytes=VMEM_LIMIT),
        name="post",
    )(x2d, onsa, og, p2d, wo, nf, wg, wu, wd, wple, npl, wpg, nfin)


def _t5_bucket(dist):
    n = jnp.maximum(dist, 0)
    max_exact = N_BUCKETS // 2
    nf = jnp.maximum(n, 1).astype(f32)
    large = max_exact + (jnp.log(nf / max_exact) / math.log(MAX_DISTANCE / max_exact)
                         * (N_BUCKETS - max_exact)).astype(jnp.int32)
    large = jnp.minimum(large, N_BUCKETS - 1)
    return jnp.where(n < max_exact, n, large)


def _bias_of(dist, valid, table, shift=None):
    vals = table[jnp.clip(dist, 0, table.shape[0] - 1)]
    if shift is not None:
        vals = vals - shift
    vals = jnp.where(valid[..., None], vals, NEG_INF)
    return jnp.moveaxis(vals, -1, 0)


def _pack_w_in(w_in):
    pad = lambda n: jnp.zeros((D_MODEL, n), w_in.dtype)
    return jnp.concatenate([w_in[:, 0:1280], w_in[:, 1304:2328], w_in[:, 2344:2856],
                            w_in[:, 1280:1304], pad(LR_LANE - 24), w_in[:, 2328:2344], pad(128 - LR_LANE - 16)],
                           axis=1).astype(bf16)


def _compress_weights(cmp_pe, cmp_w1, cmp_w2):
    w1 = cmp_w1.reshape(2, CMP_RATIO, CMP_STRIDE, HEAD_DIM, CMP_HIDDEN)
    pe = cmp_pe.reshape(2, CMP_RATIO, CMP_STRIDE, HEAD_DIM)
    w1bd = jnp.zeros((CMP_RATIO, CMP_STRIDE, 256, 256), f32)
    w2bd = jnp.zeros((256, 256), f32)
    pes = []
    for kv in range(2):
        for g in range(NSA_KV_HEADS):
            c = (kv * 2 + g) * 64
            w1bd = w1bd.at[:, :, c:c + 64, c:c + 64].set(w1[kv])
            w2bd = w2bd.at[c:c + 64, c:c + 64].set(cmp_w2[kv])
            pes.append(pe[kv])
    return jnp.concatenate(pes, axis=-1), w1bd.astype(bf16), w2bd.astype(bf16)


def _importance_matrix(n_rows, n_blocks, n_cmp, n_sub):
    a = np.zeros((n_rows, n_sub), np.float32)
    for j in range(n_blocks):
        for m in range(SLC_RATIO):
            for n in range(CMP_RATIO):
                i = SLC_RATIO * j + m - n
                if 0 <= i < n_cmp:
                    a[j, i] += 1.0
    return jnp.asarray(a, bf16)


def kernel(x_prompt, x_sample, cache_nsa_kv, cache_win_kv, state_gla, page_table, p_prompt, p_sample, norm_mix,
           w_in, cmp_pe, cmp_w1, cmp_w2, w_gk, b_gk, gla_norm, w_o, norm_ffn, w_gate, w_up, w_down, w_ple,
           norm_ple, w_ple_gate, rel_bias, norm_final):
    assert w_in.shape[0] == 1, "single-layer trunk"
    row = lambda v: v.reshape(1, -1)
    w_packed = _pack_w_in(w_in[0])
    wgk_pad = jnp.zeros((128, 256), f32).at[LR_LANE:LR_LANE + GLA_LOWRANK].set(w_gk[0]).astype(bf16)
    pe, w1bd, w2bd = _compress_weights(cmp_pe[0], cmp_w1[0], cmp_w2[0])
    post_w = (w_o[0].astype(bf16), row(norm_ffn[0]), w_gate[0].astype(bf16), w_up[0].astype(bf16),
              w_down[0].astype(bf16), w_ple[0].astype(bf16), row(norm_ple[0]), w_ple_gate[0].astype(bf16),
              row(norm_final))

    tbl = rel_bias[_t5_bucket(jnp.arange(PAST_LEN + 256, dtype=jnp.int32))]
    far = rel_bias[N_BUCKETS - 1]
    ar = lambda n: jnp.arange(n, dtype=jnp.int32)

    (qn, kvm, kvw, misc, qg, kg, vg, gg, loga, kt, vb) = _project(
        x_prompt.reshape(BATCH * SEQ, D_MODEL), row(norm_mix[0]), w_packed, wgk_pad, row(b_gk[0]),
        tm=512, seq_per_batch=SEQ)
    cmp_p = _compress_prompt(kvm, pe, w1bd, w2bd)
    d_c = ar(SEQ)[:, None] - (ar(N_CMP_P)[None, :] * CMP_STRIDE + (CMP_LEN - 1))
    bias_c = _bias_of(d_c, (d_c >= 0) & (ar(N_CMP_P)[None, :] < N_CMP_P - 1), tbl)
    kinds_w = []
    for kind in range(5):
        q0, start = (kind * TQ, 0) if kind < 4 else (WINDOW, 0)
        d = (q0 + ar(TQ))[:, None] - (start + ar(WBAND_P))[None, :]
        kinds_w.append(_bias_of(d, (d >= 0) & (d < WINDOW), tbl))
    wtab = jnp.stack(kinds_w)
    kinds_s = []
    for kind in range(4):
        d = (kind * TQ + ar(TQ))[:, None] - ar(SBAND_P)[None, :]
        kinds_s.append(_bias_of(d, d >= 0, tbl, shift=far))
    stab = jnp.stack(kinds_s)
    at_p = _importance_matrix(N_SLC_P, N_SLC_P, N_CMP_P - 1, N_CMP_P)
    onsa = _nsa_prompt(qn.reshape(BATCH, SEQ, 512), cmp_p, kt, vb, misc.reshape(BATCH, SEQ, 128),
                       bias_c, wtab, stab, at_p)
    s0_p = jnp.zeros((BATCH, GLA_HEADS, GLA_DK, GLA_DV), f32)
    sh = lambda a: a.reshape(BATCH, SEQ, a.shape[-1])
    og, s_p = _gla(sh(qg), sh(kg), sh(vg), sh(gg), sh(loga), s0_p, row(gla_norm[0]), chunk=GLA_CHUNK, tb=512)
    y_p = _post(x_prompt.reshape(BATCH * SEQ, D_MODEL), onsa.reshape(BATCH * SEQ, 512),
                og.reshape(BATCH * SEQ, 512), p_prompt[0].reshape(BATCH * SEQ, D_PLE), *post_w, tm=512)

    ms = DEC_BATCH * DEC_SEQ
    (qn_s, kvm_s, kvw_s, misc_s, qg_s, kg_s, vg_s, gg_s, loga_s) = _project(
        x_sample.reshape(ms, D_MODEL), row(norm_mix[0]), w_packed, wgk_pad, row(b_gk[0]), tm=ms)
    q5 = qn_s.reshape(DEC_BATCH, DEC_SEQ, NSA_KV_HEADS, NSA_GROUP, HEAD_DIM).transpose(0, 2, 3, 1, 4)
    q5 = q5.reshape(DEC_BATCH, NSA_KV_HEADS, 16, HEAD_DIM)
    zq = jnp.zeros_like(q5[:, 0])
    q4pad = jnp.stack([jnp.concatenate([q5[:, 0], zq], axis=-1), jnp.concatenate([zq, q5[:, 1]], axis=-1)], axis=1)
    gl = misc_s[:, 0:24].reshape(DEC_BATCH, DEC_SEQ, 3, NSA_KV_HEADS, NSA_GROUP).transpose(0, 3, 4, 1, 2)
    gl = jnp.pad(gl.reshape(DEC_BATCH, NSA_KV_HEADS, 16, 3), ((0, 0), (0, 0), (0, 0), (0, 125)))
    pos_s = PAST_LEN + ar(DEC_SEQ)

    def rows_rt(bias):
        return bias.reshape(NSA_KV_HEADS, NSA_GROUP * DEC_SEQ, bias.shape[-1])

    d_c = pos_s[:, None] - (ar(N_SUB_S)[None, :] * CMP_STRIDE + (CMP_LEN - 1))
    bc_s = rows_rt(_bias_of(d_c, (d_c >= 0) & (ar(N_SUB_S)[None, :] < N_SUB_S - 1), tbl))
    d_s = pos_s[:, None] - ar(KEYS_S)[None, :]
    bs_s = rows_rt(_bias_of(d_s, d_s >= 0, tbl))
    d_w = pos_s[:, None] - (PAST_LEN - WINDOW + ar(WBAND_S))[None, :]
    bw_s = rows_rt(_bias_of(d_w, (d_w >= 0) & (d_w < WINDOW), tbl))
    at_s = _importance_matrix(NBP_S, N_SLC_S, N_SUB_S - 1, N_SUB_S)
    expand_s = jnp.asarray(np.arange(256)[:, None] == (np.arange(KEYS_S)[None, :] // SLC_BLOCK), bf16)
    o_s = _nsa_sample(page_table, q4pad, gl, kvm_s.reshape(DEC_BATCH, DEC_SEQ, 512),
                      kvw_s.reshape(DEC_BATCH, DEC_SEQ, 256), cache_win_kv[0].reshape(DEC_BATCH, WINDOW, 256),
                      cache_nsa_kv[0], pe, w1bd, w2bd, at_s, expand_s, bc_s, bs_s, bw_s)
    onsa_s = o_s.reshape(DEC_BATCH, NSA_KV_HEADS, NSA_GROUP, DEC_SEQ, HEAD_DIM).transpose(0, 3, 1, 2, 4)
    onsa_s = onsa_s.reshape(ms, 512).astype(bf16)
    tpad = 8
    sp = lambda a: jnp.pad(a.reshape(DEC_BATCH, DEC_SEQ, a.shape[-1]), ((0, 0), (0, tpad - DEC_SEQ), (0, 0)))
    og_s, s_s = _gla(sp(qg_s), sp(kg_s), sp(vg_s), sp(gg_s), sp(loga_s), state_gla[0], row(gla_norm[0]),
                     chunk=tpad, tb=tpad)
    y_s = _post(x_sample.reshape(ms, D_MODEL), onsa_s, og_s[:, :DEC_SEQ].reshape(ms, 512),
                p_sample[0].reshape(ms, D_PLE), *post_w, tm=ms)

    new_win_s = jnp.concatenate([cache_win_kv[0][:, DEC_SEQ:],
                                 kvw_s.reshape(DEC_BATCH, DEC_SEQ, 2, NSA_KV_HEADS, HEAD_DIM)], axis=1)
    return (y_p.reshape(BATCH, SEQ, D_MODEL),
            y_s.reshape(DEC_BATCH, DEC_SEQ, D_MODEL),
            kvm.reshape(1, BATCH, SEQ, 4, NSA_KV_HEADS, HEAD_DIM),
            kvm_s.reshape(1, DEC_BATCH, DEC_SEQ, 4, NSA_KV_HEADS, HEAD_DIM),
            kvw.reshape(BATCH, SEQ, 2, NSA_KV_HEADS, HEAD_DIM)[None, :, SEQ - WINDOW:],
            new_win_s[None],
            s_p[None],
            s_s[None])
```
